```python
import jax, jax.numpy as jnp
from jax import lax
import numpy as np

D_MODEL = 1024
BATCH = 32
SEQ = 256
DEPTH = 4
DEC_BATCH = 4
DEC_SEQ = 4096
PAST_LEN = 256

F32 = jnp.float32
GRID_W = 64
Q_BLOCK = 128
LN_EPS = 1e-6
RMS_EPS = 1e-6
ROPE_THETA = 10000.0
DEEPNORM_ALPHA = (2 * DEPTH) ** 0.25
DEEPNORM_BETA = (8 * DEPTH) ** -0.25
N_MOD = 9
D_FF = 2816

LRU_WIDTH = 256
LRU_BLOCKS = 4
LRU_BLOCK_W = LRU_WIDTH // LRU_BLOCKS
CONV_W = 4
CONV_PAD_LO = 1
LRU_C = 8.0
GQA_HEADS = 8
GQA_KV_HEADS = 2
HEAD_DIM = 64
MLA_HEADS = 4
MLA_Q_RANK = 192
MLA_KV_RANK = 128
MLA_NOPE_DIM = 64
MLA_ROPE_DIM = 32
MLA_V_DIM = 64

IN_WIDTHS = (LRU_WIDTH, LRU_WIDTH, GQA_HEADS * HEAD_DIM, GQA_KV_HEADS * HEAD_DIM, GQA_KV_HEADS * HEAD_DIM, MLA_Q_RANK, MLA_KV_RANK, MLA_ROPE_DIM)
IN_COLS = 2 * LRU_WIDTH + (GQA_HEADS + 2 * GQA_KV_HEADS) * HEAD_DIM + MLA_Q_RANK + MLA_KV_RANK + MLA_ROPE_DIM
MIX_WIDTH = LRU_WIDTH + GQA_HEADS * HEAD_DIM + MLA_HEADS * MLA_V_DIM

kernel_name = 'hybrid_diffusion_lru_gqa_mla_step'


def layer_norm(x, g, b):
    xf = x.astype(F32)
    mu = jnp.mean(xf, axis=-1, keepdims=True)
    var = jnp.mean(jnp.square(xf - mu), axis=-1, keepdims=True)
    return ((xf - mu) * lax.rsqrt(var + LN_EPS) * g.astype(F32) + b.astype(F32)).astype(x.dtype)


def rms_norm(x, g):
    xf = x.astype(F32)
    return (xf * lax.rsqrt(jnp.mean(xf * xf, axis=-1, keepdims=True) + RMS_EPS) * g.astype(F32)).astype(x.dtype)


def grid_rope(num_tokens, dim):
    rows = num_tokens // GRID_W
    row = jnp.repeat(jnp.arange(rows), GRID_W).astype(F32)
    col = jnp.tile(jnp.arange(GRID_W), rows).astype(F32)
    n_freq = dim // 4
    inv = ROPE_THETA ** (-jnp.arange(n_freq, dtype=F32) / n_freq)
    ang = jnp.concatenate([row[:, None] * inv, col[:, None] * inv], axis=-1)
    return jnp.cos(ang), jnp.sin(ang)


def apply_rope(x, cos, sin):
    xf = x.astype(F32)
    half = x.shape[-1] // 2
    x1, x2 = xf[..., :half], xf[..., half:]
    c = cos[None, :, None, :]
    s = sin[None, :, None, :]
    return jnp.concatenate([x1 * c - x2 * s, x1 * s + x2 * c], axis=-1).astype(x.dtype)


def attention(q, k, v):
    b, tq, h, dq = q.shape
    kvh = k.shape[2]
    grp = h // kvh
    dv = v.shape[-1]
    scale = dq ** -0.5
    nblk = tq // Q_BLOCK
    qb = q.reshape(b, nblk, Q_BLOCK, kvh, grp, dq).transpose(1, 0, 2, 3, 4, 5)

    def block(qi):
        s = jnp.einsum('bqkgd,bskd->bkgqs', qi, k, preferred_element_type=F32) * scale
        p = jax.nn.softmax(s, axis=-1).astype(v.dtype)
        return jnp.einsum('bkgqs,bskd->bqkgd', p, v)

    o = lax.map(block, qb)
    return o.transpose(1, 0, 2, 3, 4, 5).reshape(b, tq, h * dv)


def depthwise_conv(x, w, bias):
    y = lax.conv_general_dilated(x, w[:, None, :], window_strides=(1,), padding=[(CONV_PAD_LO, CONV_W - 1 - CONV_PAD_LO)], dimension_numbers=('NWC', 'WIO', 'NWC'), feature_group_count=x.shape[-1])
    return y + bias


def rglru_scan(x, h0, lam, w_a, b_a, w_i, b_i, reverse):
    b, t, w = x.shape
    xf = x.astype(F32)
    xb = xf.reshape(b, t, LRU_BLOCKS, LRU_BLOCK_W)
    r = jax.nn.sigmoid(jnp.einsum('btnk,nkj->btnj', xb, w_a.astype(F32)).reshape(b, t, w) + b_a.astype(F32))
    i = jax.nn.sigmoid(jnp.einsum('btnk,nkj->btnj', xb, w_i.astype(F32)).reshape(b, t, w) + b_i.astype(F32))
    log_a = -LRU_C * r * jax.nn.softplus(-lam.astype(F32))
    a = jnp.exp(log_a)
    u = jnp.sqrt(-jnp.expm1(2.0 * log_a)) * (i * xf)

    def step(h, au):
        a_t, u_t = au
        h = a_t * h + u_t
        return h, h

    h_last, hs = lax.scan(step, h0.astype(F32), (jnp.swapaxes(a, 0, 1), jnp.swapaxes(u, 0, 1)), reverse=reverse)
    return jnp.swapaxes(hs, 0, 1), h_last


def split_in(z):
    out, off = [], 0
    for wdt in IN_WIDTHS:
        out.append(z[..., off:off + wdt])
        off += wdt
    return out


def swiglu(u, lp, j):
    return (jax.nn.silu(u @ lp['ffn_w_gate'][j]) * (u @ lp['ffn_w_up'][j])) @ lp['ffn_w_down'][j]


def mixer(u, lp, ropes, ctx):
    b, t, _ = u.shape
    xa, ga, qg, kg, vg, cq, ckv, kr = split_in(u @ lp['w_in'])

    xc = depthwise_conv(xa, lp['conv_w'], lp['conv_b'])
    if ctx is None:
        hf0 = jnp.zeros((b, LRU_WIDTH), F32)
        hb0 = jnp.zeros((b, LRU_WIDTH), F32)
    else:
        hf0 = ctx['lru'][:, 0]
        hb0 = ctx['lru'][:, 1]
    hf, hf_last = rglru_scan(xc, hf0, lp['lru_lambda'][0], lp['lru_w_a'][0], lp['lru_b_a'][0], lp['lru_w_i'][0], lp['lru_b_i'][0], False)
    hb, hb_last = rglru_scan(xc, hb0, lp['lru_lambda'][1], lp['lru_w_a'][1], lp['lru_b_a'][1], lp['lru_w_i'][1], lp['lru_b_i'][1], True)
    ya = jax.nn.gelu(ga) * (hf + hb).astype(u.dtype)

    q = rms_norm(qg.reshape(b, t, GQA_HEADS, HEAD_DIM), lp['q_norm'])
    k = rms_norm(kg.reshape(b, t, GQA_KV_HEADS, HEAD_DIM), lp['k_norm'])
    v = vg.reshape(b, t, GQA_KV_HEADS, HEAD_DIM)

    qc = (rms_norm(cq, lp['mla_q_norm']) @ lp['mla_w_uq']).reshape(b, t, MLA_HEADS, MLA_NOPE_DIM + MLA_ROPE_DIM)
    qc_nope, qc_rope = qc[..., :MLA_NOPE_DIM], qc[..., MLA_NOPE_DIM:]
    ckv_n = rms_norm(ckv, lp['mla_kv_norm'])
    kr = kr[:, :, None, :]

    if ctx is None:
        k_att, v_att, ckv_att, kr_att = k, v, ckv_n, kr
    else:
        cos_g, sin_g = ropes[0]
        cos_m, sin_m = ropes[1]
        q = apply_rope(q, cos_g, sin_g)
        k_lat = apply_rope(k, cos_g, sin_g)
        qc_rope = apply_rope(qc_rope, cos_m, sin_m)
        kr_lat = apply_rope(kr, cos_m, sin_m)
        k_att = jnp.concatenate([ctx['k'].astype(k.dtype), k_lat], axis=1)
        v_att = jnp.concatenate([ctx['v'].astype(v.dtype), v], axis=1)
        ckv_att = jnp.concatenate([ctx['ckv'].astype(ckv_n.dtype), ckv_n], axis=1)
        kr_att = jnp.concatenate([ctx['krope'].astype(kr.dtype)[:, :, None, :], kr_lat], axis=1)

    yb = attention(q, k_att, v_att)

    s = ckv_att.shape[1]
    k_nope = (ckv_att @ lp['mla_w_uk']).reshape(b, s, MLA_HEADS, MLA_NOPE_DIM)
    v_c = (ckv_att @ lp['mla_w_uv']).reshape(b, s, MLA_HEADS, MLA_V_DIM)
    k_c = jnp.concatenate([k_nope, jnp.broadcast_to(kr_att, (b, s, MLA_HEADS, MLA_ROPE_DIM))], axis=-1)
    q_c = jnp.concatenate([qc_nope, qc_rope], axis=-1)
    yc = attention(q_c, k_c, v_c)

    y = jnp.concatenate([ya, yb, yc], axis=-1) @ lp['w_out']
    if ctx is None:
        return y, (k, v, ckv_n, kr[:, :, 0, :], jnp.stack([hf_last, hb_last], axis=1))
    return y, None


def trunk_layer(x, cond, lp, ropes, ctx):
    mod = (jax.nn.silu(cond) @ lp['w_mod'] + lp['b_mod'])[:, None, :]
    sh1, sc1, g1, sh2, sc2, g2, sh3, sc3, g3 = jnp.split(mod, N_MOD, axis=-1)
    x = layer_norm(DEEPNORM_ALPHA * x + 0.5 * g1 * swiglu(x * (1 + sc1) + sh1, lp, 0), lp['ln_g'][0], lp['ln_b'][0])
    y, ctx_out = mixer(x * (1 + sc2) + sh2, lp, ropes, ctx)
    x = layer_norm(DEEPNORM_ALPHA * x + g2 * y, lp['ln_g'][1], lp['ln_b'][1])
    x = layer_norm(DEEPNORM_ALPHA * x + 0.5 * g3 * swiglu(x * (1 + sc3) + sh3, lp, 1), lp['ln_g'][2], lp['ln_b'][2])
    return x, ctx_out


def setup_inputs(seed: int = 0) -> dict:
    key = jax.random.key(seed)
    ks = jax.random.split(key, 40)

    def nrm(k, shape, scale):
        return jax.random.normal(k, shape, F32) * scale

    L = DEPTH
    u_a = jax.random.uniform(ks[20], (L, 2, LRU_WIDTH), F32, minval=0.9, maxval=0.999)
    s_a = u_a ** (1.0 / LRU_C)
    lru_lambda = jnp.log(s_a) - jnp.log1p(-s_a)
    return {
        'x_prompt': nrm(ks[0], (BATCH, SEQ, D_MODEL), 1.0),
        'x_sample': nrm(ks[1], (DEC_BATCH, DEC_SEQ, D_MODEL), 1.0),
        'cache_gqa_k': nrm(ks[2], (DEC_BATCH, DEPTH, PAST_LEN, GQA_KV_HEADS, HEAD_DIM), 1.0),
        'cache_gqa_v': nrm(ks[3], (DEC_BATCH, DEPTH, PAST_LEN, GQA_KV_HEADS, HEAD_DIM), 1.0),
        'cache_mla_ckv': nrm(ks[4], (DEC_BATCH, DEPTH, PAST_LEN, MLA_KV_RANK), 1.0),
        'cache_mla_krope': nrm(ks[5], (DEC_BATCH, DEPTH, PAST_LEN, MLA_ROPE_DIM), 1.0),
        'state_lru': nrm(ks[6], (DEC_BATCH, DEPTH, 2, LRU_WIDTH), 0.5),
        'c': nrm(ks[7], (DEC_BATCH, D_MODEL), 1.0),
        'c_ctx': nrm(ks[8], (D_MODEL,), 1.0),
        'w_mod': nrm(ks[9], (L, D_MODEL, N_MOD * D_MODEL), 0.5 * D_MODEL ** -0.5),
        'b_mod': nrm(ks[10], (L, N_MOD * D_MODEL), 0.02),
        'ln_g': 1.0 + nrm(ks[11], (L, 3, D_MODEL), 0.02),
        'ln_b': nrm(ks[12], (L, 3, D_MODEL), 0.02),
        'ffn_w_gate': nrm(ks[13], (L, 2, D_MODEL, D_FF), D_MODEL ** -0.5),
        'ffn_w_up': nrm(ks[14], (L, 2, D_MODEL, D_FF), D_MODEL ** -0.5),
        'ffn_w_down': nrm(ks[15], (L, 2, D_FF, D_MODEL), DEEPNORM_BETA * D_FF ** -0.5),
        'w_in': nrm(ks[16], (L, D_MODEL, IN_COLS), D_MODEL ** -0.5),
        'w_out': nrm(ks[17], (L, MIX_WIDTH, D_MODEL), DEEPNORM_BETA * MIX_WIDTH ** -0.5),
        'lru_conv_w': nrm(ks[18], (L, CONV_W, LRU_WIDTH), CONV_W ** -0.5),
        'lru_conv_b': nrm(ks[19], (L, LRU_WIDTH), 0.02),
        'lru_w_a': nrm(ks[21], (L, 2, LRU_BLOCKS, LRU_BLOCK_W, LRU_BLOCK_W), LRU_BLOCK_W ** -0.5),
        'lru_b_a': nrm(ks[22], (L, 2, LRU_WIDTH), 0.02),
        'lru_w_i': nrm(ks[23], (L, 2, LRU_BLOCKS, LRU_BLOCK_W, LRU_BLOCK_W), LRU_BLOCK_W ** -0.5),
        'lru_b_i': nrm(ks[24], (L, 2, LRU_WIDTH), 0.02),
        'lru_lambda': lru_lambda,
        'gqa_q_norm': 1.0 + nrm(ks[25], (L, HEAD_DIM), 0.02),
        'gqa_k_norm': 1.0 + nrm(ks[26], (L, HEAD_DIM), 0.02),
        'mla_q_norm': 1.0 + nrm(ks[27], (L, MLA_Q_RANK), 0.02),
        'mla_w_uq': nrm(ks[28], (L, MLA_Q_RANK, MLA_HEADS * (MLA_NOPE_DIM + MLA_ROPE_DIM)), MLA_Q_RANK ** -0.5),
        'mla_kv_norm': 1.0 + nrm(ks[29], (L, MLA_KV_RANK), 0.02),
        'mla_w_uk': nrm(ks[30], (L, MLA_KV_RANK, MLA_HEADS * MLA_NOPE_DIM), MLA_KV_RANK ** -0.5),
        'mla_w_uv': nrm(ks[31], (L, MLA_KV_RANK, MLA_HEADS * MLA_V_DIM), MLA_KV_RANK ** -0.5),
    }


def reference(x_prompt, x_sample, cache_gqa_k, cache_gqa_v, cache_mla_ckv, cache_mla_krope, state_lru, c, c_ctx, w_mod, b_mod, ln_g, ln_b, ffn_w_gate, ffn_w_up, ffn_w_down, w_in, w_out, lru_conv_w, lru_conv_b, lru_w_a, lru_b_a, lru_w_i, lru_b_i, lru_lambda, gqa_q_norm, gqa_k_norm, mla_q_norm, mla_w_uq, mla_kv_norm, mla_w_uk, mla_w_uv):
    t_lat = x_sample.shape[1]
    ropes = (grid_rope(t_lat, HEAD_DIM), grid_rope(t_lat, MLA_ROPE_DIM))
    cond_ctx = c_ctx[None, :]
    xp, xs = x_prompt, x_sample
    new_k, new_v, new_ckv, new_kr, new_lru = [], [], [], [], []
    for l in range(DEPTH):
        lp = {
            'w_mod': w_mod[l], 'b_mod': b_mod[l], 'ln_g': ln_g[l], 'ln_b': ln_b[l],
            'ffn_w_gate': ffn_w_gate[l], 'ffn_w_up': ffn_w_up[l], 'ffn_w_down': ffn_w_down[l],
            'w_in': w_in[l], 'w_out': w_out[l],
            'conv_w': lru_conv_w[l], 'conv_b': lru_conv_b[l],
            'lru_w_a': lru_w_a[l], 'lru_b_a': lru_b_a[l], 'lru_w_i': lru_w_i[l], 'lru_b_i': lru_b_i[l], 'lru_lambda': lru_lambda[l],
            'q_norm': gqa_q_norm[l], 'k_norm': gqa_k_norm[l],
            'mla_q_norm': mla_q_norm[l], 'mla_w_uq': mla_w_uq[l], 'mla_kv_norm': mla_kv_norm[l],
            'mla_w_uk': mla_w_uk[l], 'mla_w_uv': mla_w_uv[l],
        }
        xp, (k_l, v_l, ckv_l, kr_l, lru_l) = trunk_layer(xp, cond_ctx, lp, ropes, None)
        new_k.append(k_l)
        new_v.append(v_l)
        new_ckv.append(ckv_l)
        new_kr.append(kr_l)
        new_lru.append(lru_l)
        ctx = {'k': cache_gqa_k[:, l], 'v': cache_gqa_v[:, l], 'ckv': cache_mla_ckv[:, l], 'krope': cache_mla_krope[:, l], 'lru': state_lru[:, l]}
        xs, _ = trunk_layer(xs, c, lp, ropes, ctx)
    return (xp, xs, jnp.stack(new_k, axis=1), jnp.stack(new_v, axis=1), jnp.stack(new_ckv, axis=1), jnp.stack(new_kr, axis=1), jnp.stack(new_lru, axis=1))
```

```python
import functools

import jax
import jax.numpy as jnp
from jax import lax
from jax.experimental import pallas as pl
from jax.experimental.pallas import tpu as pltpu

F32 = jnp.float32
BF16 = jnp.bfloat16

D_MODEL = 1024
BATCH = 32
SEQ = 256
DEPTH = 4
DEC_BATCH = 4
DEC_SEQ = 4096
PAST_LEN = 256
GRID_W = 64
LN_EPS = 1e-6
RMS_EPS = 1e-6
ROPE_THETA = 10000.0
DEEPNORM_ALPHA = (2 * DEPTH) ** 0.25
N_MOD = 9
D_FF = 2816
LRU_WIDTH = 256
LRU_BLOCKS = 4
LRU_BLOCK_W = LRU_WIDTH // LRU_BLOCKS
CONV_W = 4
LRU_C = 8.0
GQA_HEADS = 8
GQA_KV_HEADS = 2
GQA_GROUP = GQA_HEADS // GQA_KV_HEADS
HEAD_DIM = 64
MLA_HEADS = 4
MLA_Q_RANK = 192
MLA_KV_RANK = 128
MLA_NOPE_DIM = 64
MLA_ROPE_DIM = 32
MLA_V_DIM = 64
IN_WIDTHS = (LRU_WIDTH, LRU_WIDTH, GQA_HEADS * HEAD_DIM, GQA_KV_HEADS * HEAD_DIM,
             GQA_KV_HEADS * HEAD_DIM, MLA_Q_RANK, MLA_KV_RANK, MLA_ROPE_DIM)

LANES = 128
SUBLANES = 8
N_CTX = BATCH * SEQ
N_LAT = DEC_BATCH * DEC_SEQ
N_TOK = N_CTX + N_LAT
N_COND = 8
CTX_COND_ROW = DEC_BATCH
TM = 512
N_CTX_TILES = N_CTX // TM
LAT_TILES_PER_SEQ = DEC_SEQ // TM
N_TILES = N_TOK // TM
FF_CHUNK = 1408
IN_COLS_PAD = 1792
MLA_PAD = LANES
MLA_ROPE_OFF = MLA_NOPE_DIM
LRU_CHUNK = 512
TQ_GQA = 128
TQ_MLA = 512
VMEM_LIMIT = 52 * 1024 * 1024


def _dot(a, b):
    return jnp.dot(a, b, preferred_element_type=F32)


def _layer_norm(y, g, b):
    mu = jnp.mean(y, axis=-1, keepdims=True)
    d = y - mu
    var = jnp.mean(d * d, axis=-1, keepdims=True)
    return d * lax.rsqrt(var + LN_EPS) * g + b


def _cond_row(i):
    return jnp.where(i < N_CTX_TILES, CTX_COND_ROW, (i - N_CTX_TILES) // LAT_TILES_PER_SEQ)


def _pos_block(i):
    return jnp.where(i < N_CTX_TILES, LAT_TILES_PER_SEQ, (i - N_CTX_TILES) % LAT_TILES_PER_SEQ)


def _const_spec(shape):
    nd = len(shape)
    return pl.BlockSpec(shape, lambda *_: (0,) * nd, pipeline_mode=pl.Buffered(1))


def _params(sem):
    return pltpu.CompilerParams(dimension_semantics=sem, vmem_limit_bytes=VMEM_LIMIT)


MOD_TN = 1152


def _mod_kernel(cond_ref, w_ref, b_ref, o_ref):
    c = cond_ref[...]
    s = (c * jax.nn.sigmoid(c)).astype(BF16)
    o_ref[...] = _dot(s, w_ref[...].astype(BF16)) + b_ref[...]


def _modulation(cond, w_mod, b_mod):
    n = N_MOD * D_MODEL
    return pl.pallas_call(
        _mod_kernel,
        grid=(DEPTH, n // MOD_TN),
        in_specs=[
            pl.BlockSpec((N_COND, D_MODEL), lambda l, j: (0, 0)),
            pl.BlockSpec((None, D_MODEL, MOD_TN), lambda l, j: (l, 0, j)),
            pl.BlockSpec((None, 1, MOD_TN), lambda l, j: (l, 0, j)),
        ],
        out_specs=pl.BlockSpec((None, N_COND, MOD_TN), lambda l, j: (l, 0, j)),
        out_shape=jax.ShapeDtypeStruct((DEPTH, N_COND, n), F32),
        compiler_params=_params(("arbitrary", "arbitrary")),
        name="modulation",
    )(cond, w_mod, b_mod.reshape(DEPTH, 1, n))


def _ffn_kernel(x_ref, mod_ref, wg_ref, wu_ref, wd_ref, lng_ref, lnb_ref, o_ref, *, sub):
    x = x_ref[...]
    mod = mod_ref[...]
    sh, sc, g = mod[3 * sub:3 * sub + 1], mod[3 * sub + 1:3 * sub + 2], mod[3 * sub + 2:3 * sub + 3]
    u = (x * (1.0 + sc) + sh).astype(BF16)
    y = jnp.zeros((TM, D_MODEL), F32)
    for c in range(D_FF // FF_CHUNK):
        cols = slice(c * FF_CHUNK, (c + 1) * FF_CHUNK)
        a = _dot(u, wg_ref[:, cols])
        b = _dot(u, wu_ref[:, cols])
        h = (a * jax.nn.sigmoid(a) * b).astype(BF16)
        y = y + _dot(h, wd_ref[cols, :])
    z = DEEPNORM_ALPHA * x + (0.5 * g) * y
    o_ref[...] = _layer_norm(z, lng_ref[sub:sub + 1], lnb_ref[sub:sub + 1])


def _ffn(x, mod, wg, wu, wd, ln_g, ln_b, layer, sub, widx):
    return pl.pallas_call(
        functools.partial(_ffn_kernel, sub=sub),
        grid=(N_TILES,),
        in_specs=[
            pl.BlockSpec((TM, D_MODEL), lambda i: (i, 0)),
            pl.BlockSpec((None, None, N_MOD, D_MODEL), lambda i: (layer, _cond_row(i), 0, 0)),
            pl.BlockSpec((None, None, D_MODEL, D_FF), lambda i: (layer, widx, 0, 0),
                         pipeline_mode=pl.Buffered(1)),
            pl.BlockSpec((None, None, D_MODEL, D_FF), lambda i: (layer, widx, 0, 0),
                         pipeline_mode=pl.Buffered(1)),
            pl.BlockSpec((None, None, D_FF, D_MODEL), lambda i: (layer, widx, 0, 0),
                         pipeline_mode=pl.Buffered(1)),
            pl.BlockSpec((None, 3, D_MODEL), lambda i: (layer, 0, 0)),
            pl.BlockSpec((None, 3, D_MODEL), lambda i: (layer, 0, 0)),
        ],
        out_specs=pl.BlockSpec((TM, D_MODEL), lambda i: (i, 0)),
        out_shape=jax.ShapeDtypeStruct((N_TOK, D_MODEL), F32),
        compiler_params=_params(("arbitrary",)),
        name="swiglu_ln",
    )(x, mod, wg, wu, wd, ln_g, ln_b)


def _group_mean_sq(x, gmat, group):
    x2 = x * x
    hi = x2.astype(BF16)
    lo = (x2 - hi.astype(F32)).astype(BF16)
    return (_dot(hi, gmat) + _dot(lo, gmat)) * (1.0 / group)


def _rope_tile(x, c, sa, sb, shift):
    return x * c + pltpu.roll(x, LANES - shift, 1) * sa + pltpu.roll(x, shift, 1) * sb


def _rope(x, c, sa, sb, shift):
    tiles = [
        _rope_tile(x[:, t * LANES:(t + 1) * LANES], c, sa, sb, shift)
        for t in range(x.shape[1] // LANES)
    ]
    return tiles[0] if len(tiles) == 1 else jnp.concatenate(tiles, axis=1)


def _inproj_kernel(x_ref, mod_ref, rope_ref, w_in_ref, g512_ref, qn_ref, kn_ref, cqn_ref, ckvn_ref,
                   w_uq_ref, w_uk_ref, w_uv_ref,
                   xa_ref, gg_ref, q_ref, kt_ref, v_ref, qc_ref, kct_ref, vc_ref, ctx_ref):
    x = x_ref[...]
    mod = mod_ref[...]
    u = (x * (1.0 + mod[4:5]) + mod[3:4]).astype(BF16)
    z = _dot(u, w_in_ref[...])
    cg, sag, sbg = rope_ref[0], rope_ref[1], rope_ref[2]
    cm, sam, sbm = rope_ref[3], rope_ref[4], rope_ref[5]

    xa_ref[...] = z[:, 0:256]
    gg_ref[...] = jax.nn.gelu(z[:, 256:512])

    q = z[:, 512:1024]
    q = q * lax.rsqrt(_group_mean_sq(q, g512_ref[...], HEAD_DIM) + RMS_EPS) * qn_ref[...]
    q = _rope(q, cg, sag, sbg, HEAD_DIM // 2) * (HEAD_DIM ** -0.5)
    q_ref[...] = q.astype(BF16)
    k = z[:, 1024:1152]
    k = k * lax.rsqrt(_group_mean_sq(k, g512_ref[0:LANES, 0:LANES], HEAD_DIM) + RMS_EPS) * kn_ref[...]
    kt_ref[...] = _rope(k, cg, sag, sbg, HEAD_DIM // 2).T.astype(BF16)
    v = z[:, 1152:1280]
    v_ref[...] = v.astype(BF16)

    ckv = z[:, 1280:1408]
    ckv = ckv * lax.rsqrt(jnp.mean(ckv * ckv, axis=-1, keepdims=True) + RMS_EPS) * ckvn_ref[...]
    cq = z[:, 1408:1664]
    cq_ms = jnp.sum(cq * cq, axis=-1, keepdims=True) * (1.0 / MLA_Q_RANK)
    cq = cq * lax.rsqrt(cq_ms + RMS_EPS) * cqn_ref[...]
    qc = _dot(cq.astype(BF16), w_uq_ref[...])
    qc = _rope(qc, cm, sam, sbm, MLA_ROPE_DIM // 2) * ((MLA_NOPE_DIM + MLA_ROPE_DIM) ** -0.5)
    qc_ref[...] = qc.astype(BF16)
    kr = z[:, 1664:1792]
    kr_rot = _rope_tile(kr, cm, sam, sbm, MLA_ROPE_DIM // 2)
    ckv_b = ckv.astype(BF16)
    kc = _dot(ckv_b, w_uk_ref[...])
    kc = jnp.concatenate(
        [kc[:, h * MLA_PAD:(h + 1) * MLA_PAD] + kr_rot for h in range(MLA_HEADS)], axis=1)
    kct_ref[...] = kc.T.astype(BF16)
    vc_ref[...] = _dot(ckv_b, w_uv_ref[...]).astype(BF16)

    ctx_ref[...] = jnp.concatenate([k, v, ckv, kr], axis=1)


def _inproj(x, mod, rope_tab, w_in, g512, qn, kn, cqn, ckvn, w_uq, w_uk, w_uv, layer):
    tok = lambda w: pl.BlockSpec((TM, w), lambda i: (i, 0))
    tok_t = lambda w: pl.BlockSpec((w, TM), lambda i: (0, i))
    lay = lambda *s: pl.BlockSpec((None,) + s, lambda i: (layer,) + (0,) * len(s),
                                  pipeline_mode=pl.Buffered(1))
    n_heads_pad = MLA_HEADS * MLA_PAD
    return pl.pallas_call(
        _inproj_kernel,
        grid=(N_TILES,),
        in_specs=[
            tok(D_MODEL),
            pl.BlockSpec((None, None, N_MOD, D_MODEL), lambda i: (layer, _cond_row(i), 0, 0)),
            pl.BlockSpec((6, TM, LANES), lambda i: (0, _pos_block(i), 0)),
            lay(D_MODEL, IN_COLS_PAD),
            _const_spec((512, 512)),
            lay(1, 512), lay(1, LANES), lay(1, 256), lay(1, LANES),
            lay(256, n_heads_pad), lay(MLA_KV_RANK, n_heads_pad), lay(MLA_KV_RANK, n_heads_pad),
        ],
        out_specs=[
            tok(LRU_WIDTH), tok(LRU_WIDTH), tok(512), tok_t(LANES), tok(LANES),
            tok(n_heads_pad), tok_t(n_heads_pad), tok(n_heads_pad),
            pl.BlockSpec((TM, 512), lambda i: (jnp.minimum(i, N_CTX_TILES), 0)),
        ],
        out_shape=[
            jax.ShapeDtypeStruct((N_TOK, LRU_WIDTH), F32),
            jax.ShapeDtypeStruct((N_TOK, LRU_WIDTH), F32),
            jax.ShapeDtypeStruct((N_TOK, 512), BF16),
            jax.ShapeDtypeStruct((LANES, N_TOK), BF16),
            jax.ShapeDtypeStruct((N_TOK, LANES), BF16),
            jax.ShapeDtypeStruct((N_TOK, n_heads_pad), BF16),
            jax.ShapeDtypeStruct((n_heads_pad, N_TOK), BF16),
            jax.ShapeDtypeStruct((N_TOK, n_heads_pad), BF16),
            jax.ShapeDtypeStruct((N_CTX + TM, 512), F32),
        ],
        compiler_params=_params(("arbitrary",)),
        name="in_proj",
    )(x, mod, rope_tab, w_in, g512, qn, kn, cqn, ckvn, w_uq, w_uk, w_uv)


def _cache_kernel(k_ref, v_ref, ckv_ref, kr_ref, w_uk_ref, w_uv_ref, place_ref,
                  kt_ref, vo_ref, kct_ref, vc_ref):
    kt_ref[...] = k_ref[...].T.astype(BF16)
    vo_ref[...] = v_ref[...].astype(BF16)
    ckv_b = ckv_ref[...].astype(BF16)
    kc = _dot(ckv_b, w_uk_ref[...]) + _dot(kr_ref[...].astype(BF16), place_ref[...])
    kct_ref[...] = kc.T.astype(BF16)
    vc_ref[...] = _dot(ckv_b, w_uv_ref[...]).astype(BF16)


def _cache_prep(cache_k, cache_v, cache_ckv, cache_kr, w_uk, w_uv, place):
    n_heads_pad = MLA_HEADS * MLA_PAD
    seq = lambda w: pl.BlockSpec((None, None, PAST_LEN, w), lambda l, b: (b, l, 0, 0))
    out = lambda r, c: pl.BlockSpec((None, None, r, c), lambda l, b: (l, b, 0, 0))
    return pl.pallas_call(
        _cache_kernel,
        grid=(DEPTH, DEC_BATCH),
        in_specs=[
            seq(LANES), seq(LANES), seq(MLA_KV_RANK), seq(MLA_ROPE_DIM),
            pl.BlockSpec((None, MLA_KV_RANK, n_heads_pad), lambda l, b: (l, 0, 0)),
            pl.BlockSpec((None, MLA_KV_RANK, n_heads_pad), lambda l, b: (l, 0, 0)),
            pl.BlockSpec((MLA_ROPE_DIM, n_heads_pad), lambda l, b: (0, 0)),
        ],
        out_specs=[out(LANES, PAST_LEN), out(PAST_LEN, LANES),
                   out(n_heads_pad, PAST_LEN), out(PAST_LEN, n_heads_pad)],
        out_shape=[
            jax.ShapeDtypeStruct((DEPTH, DEC_BATCH, LANES, PAST_LEN), BF16),
            jax.ShapeDtypeStruct((DEPTH, DEC_BATCH, PAST_LEN, LANES), BF16),
            jax.ShapeDtypeStruct((DEPTH, DEC_BATCH, n_heads_pad, PAST_LEN), BF16),
            jax.ShapeDtypeStruct((DEPTH, DEC_BATCH, PAST_LEN, n_heads_pad), BF16),
        ],
        compiler_params=_params(("arbitrary", "arbitrary")),
        name="cache_prep",
    )(cache_k, cache_v, cache_ckv, cache_kr, w_uk, w_uv, place)


def _lru_kernel(xa_ref, gg_ref, h0_ref, cw_ref, cb_ref, wg_ref, bg_ref, lam_ref,
                ya_ref, hl_ref, hf_scr, a_scr, u_scr, *, seq_len, chunk):
    n_chunks = seq_len // chunk
    n_rows = chunk // SUBLANES
    row = lax.broadcasted_iota(jnp.int32, (chunk, LRU_WIDTH), 0)
    sub = row & (SUBLANES - 1)
    cw = cw_ref[...]
    lam = lam_ref[...]
    softplus_neg = jnp.maximum(-lam, 0.0) + jnp.log1p(jnp.exp(-jnp.abs(lam)))

    def conv(c):
        start = pl.multiple_of(c * chunk, chunk)
        x = xa_ref[pl.ds(start, chunk), :]
        prev = xa_ref[pl.ds(pl.multiple_of(jnp.maximum(start - SUBLANES, 0), SUBLANES), SUBLANES), :]
        nxt = xa_ref[pl.ds(pl.multiple_of(jnp.minimum(start + chunk, seq_len - SUBLANES), SUBLANES),
                           SUBLANES), :]
        p1 = jnp.where(c > 0, prev[SUBLANES - 1:SUBLANES], 0.0)
        n0 = jnp.where(c < n_chunks - 1, nxt[0:1], 0.0)
        n1 = jnp.where(c < n_chunks - 1, nxt[1:2], 0.0)
        xm1 = jnp.where(row == 0, p1, pltpu.roll(x, 1, 0))
        xp1 = jnp.where(row == chunk - 1, n0, pltpu.roll(x, chunk - 1, 0))
        xp2 = jnp.where(row == chunk - 2, n0, jnp.where(row == chunk - 1, n1, pltpu.roll(x, chunk - 2, 0)))
        return cw[0:1] * xm1 + cw[1:2] * x + cw[2:3] * xp1 + cw[3:4] * xp2 + cb_ref[...]

    def gates(xc, d):
        g = _dot(xc.astype(BF16), wg_ref[:, 2 * d * LRU_WIDTH:2 * (d + 1) * LRU_WIDTH])
        g = g + bg_ref[:, 2 * d * LRU_WIDTH:2 * (d + 1) * LRU_WIDTH]
        r = jax.nn.sigmoid(g[:, :LRU_WIDTH])
        i = jax.nn.sigmoid(g[:, LRU_WIDTH:])
        log_a = (-LRU_C * r) * softplus_neg[d:d + 1]
        a = jnp.exp(log_a)
        u = jnp.sqrt(-jnp.tanh(log_a) * (a * a + 1.0)) * (i * xc)
        return a, u

    def tile_scan(a, u, reverse):
        for step in (1, 2, 4):
            if reverse:
                keep = sub < SUBLANES - step
                shift = chunk - step
            else:
                keep = sub >= step
                shift = step
            a_n = jnp.where(keep, pltpu.roll(a, shift, 0), 1.0)
            u_n = jnp.where(keep, pltpu.roll(u, shift, 0), 0.0)
            u = u + a * u_n
            a = a * a_n
        return a, u

    def carry_pass(h, out_ref, out_start, reverse):
        def body(t, h):
            tt = (n_rows - 1 - t) if reverse else t
            r0 = pl.multiple_of(tt * SUBLANES, SUBLANES)
            rows = u_scr[pl.ds(r0, SUBLANES), :] + a_scr[pl.ds(r0, SUBLANES), :] * h
            out_ref[pl.ds(pl.multiple_of(out_start + r0, SUBLANES), SUBLANES), :] = rows
            return rows[0:1] if reverse else rows[SUBLANES - 1:SUBLANES]
        return lax.fori_loop(0, n_rows, body, h, unroll=8)

    def fwd_chunk(c, h):
        a, u = tile_scan(*gates(conv(c), 0), reverse=False)
        a_scr[...] = a
        u_scr[...] = u
        return carry_pass(h, hf_scr, pl.multiple_of(c * chunk, chunk), reverse=False)

    def bwd_chunk(j, h):
        c = n_chunks - 1 - j
        start = pl.multiple_of(c * chunk, chunk)
        a, u = tile_scan(*gates(conv(c), 1), reverse=True)
        a_scr[...] = a
        u_scr[...] = u
        h = carry_pass(h, a_scr, 0, reverse=True)
        hsum = hf_scr[pl.ds(start, chunk), :] + a_scr[...]
        ya_ref[pl.ds(start, chunk), :] = (gg_ref[pl.ds(start, chunk), :] * hsum).astype(BF16)
        return h

    hf = lax.fori_loop(0, n_chunks, fwd_chunk, h0_ref[0:1, :])
    hb = lax.fori_loop(0, n_chunks, bwd_chunk, h0_ref[1:2, :])
    hl_ref[0:1, :] = hf
    hl_ref[1:2, :] = hb


def _lru(xa, gg, h0, cw, cb, wg, bg, lam, layer, n_seq, seq_len, row_block0):
    chunk = min(seq_len, LRU_CHUNK)
    lay = lambda *s: pl.BlockSpec((None,) + s, lambda b: (layer,) + (0,) * len(s))
    seq = pl.BlockSpec((seq_len, LRU_WIDTH), lambda b: (row_block0 + b, 0))
    return pl.pallas_call(
        functools.partial(_lru_kernel, seq_len=seq_len, chunk=chunk),
        grid=(n_seq,),
        in_specs=[
            seq, seq,
            pl.BlockSpec((None, 2, LRU_WIDTH), lambda b: (b, 0, 0)),
            lay(CONV_W, LRU_WIDTH), lay(1, LRU_WIDTH), lay(LRU_WIDTH, 4 * LRU_WIDTH),
            lay(1, 4 * LRU_WIDTH), lay(2, LRU_WIDTH),
        ],
        out_specs=[
            pl.BlockSpec((seq_len, LRU_WIDTH), lambda b: (b, 0)),
            pl.BlockSpec((None, 2, LRU_WIDTH), lambda b: (b, 0, 0)),
        ],
        out_shape=[
            jax.ShapeDtypeStruct((n_seq * seq_len, LRU_WIDTH), BF16),
            jax.ShapeDtypeStruct((n_seq, 2, LRU_WIDTH), F32),
        ],
        scratch_shapes=[
            pltpu.VMEM((seq_len, LRU_WIDTH), F32),
            pltpu.VMEM((chunk, LRU_WIDTH), F32),
            pltpu.VMEM((chunk, LRU_WIDTH), F32),
        ],
        compiler_params=_params(("arbitrary",)),
        name="rg_lru",
    )(xa, gg, h0, cw, cb, wg, bg, lam)


def _attn_kernel(*refs, tq, n_stack, has_cache):
    if has_cache:
        q_ref, kt_ref, v_ref, ktc_ref, vc_ref, o_ref = refs
    else:
        q_ref, kt_ref, v_ref, o_ref = refs
    q = q_ref[...]
    if n_stack > 1:
        w = q.shape[1] // n_stack
        q = jnp.concatenate([q[:, j * w:(j + 1) * w] for j in range(n_stack)], axis=0)
    s = _dot(q, kt_ref[...])
    m = jnp.max(s, axis=-1, keepdims=True)
    if has_cache:
        s_c = _dot(q, ktc_ref[...])
        m = jnp.maximum(m, jnp.max(s_c, axis=-1, keepdims=True))
    p = jnp.exp(s - m)
    l = jnp.sum(p, axis=-1, keepdims=True)
    o = _dot(p.astype(BF16), v_ref[...])
    if has_cache:
        p_c = jnp.exp(s_c - m)
        l = l + jnp.sum(p_c, axis=-1, keepdims=True)
        o = o + _dot(p_c.astype(BF16), vc_ref[...])
    o = o * (1.0 / l)
    if n_stack > 1:
        grp = pl.program_id(1)
        lane = lax.broadcasted_iota(jnp.int32, o.shape, 1)
        o = jnp.where(lane // w == grp, o, 0.0)
        o = o[:, :w] + o[:, w:]
        o = jnp.concatenate([o[j * tq:(j + 1) * tq] for j in range(n_stack)], axis=1)
    o_ref[...] = o.astype(BF16)


def _attention(q, kt, v, cache, *, n_seq, seq_len, row0, tq, n_groups, n_stack, layer):
    n_q = seq_len // tq
    q_w = q.shape[1] // n_groups
    k_rows = kt.shape[0] // n_groups
    qb0, sb0 = row0 // tq, row0 // seq_len
    v_spec = (pl.BlockSpec((seq_len, LANES), lambda b, g, i: (sb0 + b, 0)) if n_stack > 1
              else pl.BlockSpec((seq_len, LANES), lambda b, g, i: (sb0 + b, g)))
    in_specs = [
        pl.BlockSpec((tq, q_w), lambda b, g, i: (qb0 + b * n_q + i, g)),
        pl.BlockSpec((k_rows, seq_len), lambda b, g, i: (g, sb0 + b)),
        v_spec,
    ]
    args = [q, kt, v]
    if cache is not None:
        ktc, vc = cache
        in_specs += [
            pl.BlockSpec((None, None, k_rows, PAST_LEN), lambda b, g, i: (layer, b, g, 0)),
            (pl.BlockSpec((None, None, PAST_LEN, LANES), lambda b, g, i: (layer, b, 0, 0)) if n_stack > 1
             else pl.BlockSpec((None, None, PAST_LEN, LANES), lambda b, g, i: (layer, b, 0, g))),
        ]
        args += [ktc, vc]
    return pl.pallas_call(
        functools.partial(_attn_kernel, tq=tq, n_stack=n_stack, has_cache=cache is not None),
        grid=(n_seq, n_groups, n_q),
        in_specs=in_specs,
        out_specs=pl.BlockSpec((tq, q_w), lambda b, g, i: (b * n_q + i, g)),
        out_shape=jax.ShapeDtypeStruct((n_seq * seq_len, q.shape[1]), BF16),
        compiler_params=_params(("arbitrary", "arbitrary", "arbitrary")),
        name="attention",
    )(*args)


def _outproj_kernel(x_ref, mod_ref, ya_c, ya_l, yb_c, yb_l, yc_c, yc_l, wa_ref, wb_ref, wc_ref,
                    lng_ref, lnb_ref, o_ref):
    is_ctx = pl.program_id(0) < N_CTX_TILES
    pick = lambda c, l: jnp.where(is_ctx, c[...], l[...])
    y = (_dot(pick(ya_c, ya_l), wa_ref[...]) + _dot(pick(yb_c, yb_l), wb_ref[...])
         + _dot(pick(yc_c, yc_l), wc_ref[...]))
    z = DEEPNORM_ALPHA * x_ref[...] + mod_ref[5:6, :] * y
    o_ref[...] = _layer_norm(z, lng_ref[1:2], lnb_ref[1:2])


def _outproj(x, mod, ya, yb, yc, wa, wb, wc, ln_g, ln_b, layer):
    ctx = lambda w: pl.BlockSpec((TM, w), lambda i: (jnp.minimum(i, N_CTX_TILES - 1), 0))
    lat = lambda w: pl.BlockSpec((TM, w), lambda i: (jnp.maximum(i - N_CTX_TILES, 0), 0))
    lay = lambda *s: pl.BlockSpec((None,) + s, lambda i: (layer,) + (0,) * len(s),
                                  pipeline_mode=pl.Buffered(1))
    n_heads_pad = MLA_HEADS * MLA_PAD
    return pl.pallas_call(
        _outproj_kernel,
        grid=(N_TILES,),
        in_specs=[
            pl.BlockSpec((TM, D_MODEL), lambda i: (i, 0)),
            pl.BlockSpec((None, None, N_MOD, D_MODEL), lambda i: (layer, _cond_row(i), 0, 0)),
            ctx(LRU_WIDTH), lat(LRU_WIDTH), ctx(512), lat(512), ctx(n_heads_pad), lat(n_heads_pad),
            lay(LRU_WIDTH, D_MODEL), lay(512, D_MODEL), lay(n_heads_pad, D_MODEL),
            lay(3, D_MODEL), lay(3, D_MODEL),
        ],
        out_specs=pl.BlockSpec((TM, D_MODEL), lambda i: (i, 0)),
        out_shape=jax.ShapeDtypeStruct((N_TOK, D_MODEL), F32),
        compiler_params=_params(("arbitrary",)),
        name="out_proj_ln",
    )(x, mod, ya[0], ya[1], yb[0], yb[1], yc[0], yc[1], wa, wb, wc, ln_g, ln_b)


def _rope_tables():
    t = jnp.arange(DEC_SEQ)
    row = (t // GRID_W).astype(F32)
    col = (t % GRID_W).astype(F32)

    def cos_sin(dim):
        n_freq = dim // 4
        inv = ROPE_THETA ** (-jnp.arange(n_freq, dtype=F32) / n_freq)
        ang = jnp.concatenate([row[:, None] * inv, col[:, None] * inv], axis=-1)
        return jnp.cos(ang), jnp.sin(ang)

    cg, sg = cos_sin(HEAD_DIM)
    zg = jnp.zeros_like(sg)
    gqa = [jnp.tile(cg, (1, 4)), jnp.tile(jnp.concatenate([-sg, zg], 1), (1, 2)),
           jnp.tile(jnp.concatenate([zg, sg], 1), (1, 2))]
    cm, sm = cos_sin(MLA_ROPE_DIM)
    zm = jnp.zeros_like(sm)
    one_lo = jnp.ones((DEC_SEQ, MLA_ROPE_OFF), F32)
    zero_lo = jnp.zeros((DEC_SEQ, MLA_ROPE_OFF), F32)
    pad_w = MLA_PAD - MLA_ROPE_OFF - MLA_ROPE_DIM
    one_hi = jnp.ones((DEC_SEQ, pad_w), F32)
    zero_hi = jnp.zeros((DEC_SEQ, pad_w), F32)
    mla = [jnp.concatenate([one_lo, cm, cm, one_hi], 1),
           jnp.concatenate([zero_lo, -sm, zm, zero_hi], 1),
           jnp.concatenate([zero_lo, zm, sm, zero_hi], 1)]
    ident = [jnp.ones((TM, LANES), F32), jnp.zeros((TM, LANES), F32), jnp.zeros((TM, LANES), F32)]
    tabs = [jnp.concatenate([tab, idn], 0) for tab, idn in zip(gqa + mla, ident + ident)]
    return jnp.stack(tabs, 0)


def _pad_heads(w, real, pad_to):
    l, k, _ = w.shape
    w = w.reshape(l, k, -1, real)
    return jnp.pad(w, ((0, 0), (0, 0), (0, 0), (0, pad_to - real))).reshape(l, k, -1)


def _block_diag(w):
    eye = jnp.eye(LRU_BLOCKS, dtype=w.dtype)
    full = jnp.einsum('ldnkj,nm->ldnkmj', w, eye)
    return full.reshape(w.shape[0], 2, LRU_WIDTH, LRU_WIDTH)


def kernel(x_prompt, x_sample, cache_gqa_k, cache_gqa_v, cache_mla_ckv, cache_mla_krope, state_lru, c, c_ctx, w_mod, b_mod, ln_g, ln_b, ffn_w_gate, ffn_w_up, ffn_w_down, w_in, w_out, lru_conv_w, lru_conv_b, lru_w_a, lru_b_a, lru_w_i, lru_b_i, lru_lambda, gqa_q_norm, gqa_k_norm, mla_q_norm, mla_w_uq, mla_kv_norm, mla_w_uk, mla_w_uv):
    L = DEPTH
    n_heads_pad = MLA_HEADS * MLA_PAD

    wg, wu, wd = ffn_w_gate.astype(BF16), ffn_w_up.astype(BF16), ffn_w_down.astype(BF16)
    xa_w, ga_w, q_w, k_w, v_w, cq_w, ckv_w, kr_w = jnp.split(
        w_in, [sum(IN_WIDTHS[:i + 1]) for i in range(len(IN_WIDTHS) - 1)], axis=-1)
    zcols = lambda n: jnp.zeros((L, D_MODEL, n), F32)
    w_in_p = jnp.concatenate(
        [xa_w, ga_w, q_w, k_w, v_w, ckv_w, cq_w, zcols(256 - MLA_Q_RANK),
         zcols(MLA_ROPE_OFF), kr_w, zcols(MLA_PAD - MLA_ROPE_OFF - MLA_ROPE_DIM)], axis=-1).astype(BF16)
    w_uq_p = _pad_heads(jnp.pad(mla_w_uq, ((0, 0), (0, 256 - MLA_Q_RANK), (0, 0))),
                        MLA_NOPE_DIM + MLA_ROPE_DIM, MLA_PAD).astype(BF16)
    w_uk_p = _pad_heads(mla_w_uk, MLA_NOPE_DIM, MLA_PAD).astype(BF16)
    w_uv_p = _pad_heads(mla_w_uv, MLA_V_DIM, MLA_PAD).astype(BF16)
    wo_a = w_out[:, :LRU_WIDTH].astype(BF16)
    wo_b = w_out[:, LRU_WIDTH:LRU_WIDTH + 512].astype(BF16)
    wo_c = jnp.pad(w_out[:, LRU_WIDTH + 512:].reshape(L, MLA_HEADS, MLA_V_DIM, D_MODEL),
                   ((0, 0), (0, 0), (0, MLA_PAD - MLA_V_DIM), (0, 0))).reshape(L, n_heads_pad, D_MODEL).astype(BF16)
    qn = jnp.tile(gqa_q_norm, (1, GQA_HEADS)).reshape(L, 1, 512)
    kn = jnp.tile(gqa_k_norm, (1, GQA_KV_HEADS)).reshape(L, 1, LANES)
    cqn = jnp.pad(mla_q_norm, ((0, 0), (0, 256 - MLA_Q_RANK))).reshape(L, 1, 256)
    ckvn = mla_kv_norm.reshape(L, 1, MLA_KV_RANK)
    g512 = jnp.kron(jnp.eye(512 // HEAD_DIM, dtype=F32), jnp.ones((HEAD_DIM, HEAD_DIM), F32)).astype(BF16)
    place = jnp.tile(jnp.pad(jnp.eye(MLA_ROPE_DIM, dtype=F32),
                             ((0, 0), (MLA_ROPE_OFF, MLA_PAD - MLA_ROPE_OFF - MLA_ROPE_DIM))),
                     (1, MLA_HEADS)).astype(BF16)
    wa_d, wi_d = _block_diag(lru_w_a), _block_diag(lru_w_i)
    w_gate = jnp.concatenate([wa_d[:, 0], wi_d[:, 0], wa_d[:, 1], wi_d[:, 1]], axis=-1).astype(BF16)
    b_gate = jnp.concatenate([lru_b_a[:, 0], lru_b_i[:, 0], lru_b_a[:, 1], lru_b_i[:, 1]],
                             axis=-1).reshape(L, 1, 4 * LRU_WIDTH)
    conv_b = lru_conv_b.reshape(L, 1, LRU_WIDTH)
    rope_tab = _rope_tables()
    cond = jnp.concatenate([c, c_ctx[None, :], jnp.zeros((N_COND - DEC_BATCH - 1, D_MODEL), F32)], axis=0)
    h0_ctx = jnp.zeros((BATCH, 2, LRU_WIDTH), F32)

    mod = _modulation(cond, w_mod, b_mod).reshape(L, N_COND, N_MOD, D_MODEL)
    ktc, vcache, kctc, vcc = _cache_prep(
        cache_gqa_k.reshape(DEC_BATCH, L, PAST_LEN, LANES), cache_gqa_v.reshape(DEC_BATCH, L, PAST_LEN, LANES),
        cache_mla_ckv, cache_mla_krope, w_uk_p, w_uv_p, place)

    x = jnp.concatenate([x_prompt.reshape(N_CTX, D_MODEL), x_sample.reshape(N_LAT, D_MODEL)], axis=0)
    new_k, new_v, new_ckv, new_kr, new_lru = [], [], [], [], []
    for l in range(L):
        x = _ffn(x, mod, wg, wu, wd, ln_g, ln_b, l, 0, 0)
        xa, gg, q, kt, v, qc, kct, vc, ctx_out = _inproj(
            x, mod, rope_tab, w_in_p, g512, qn, kn, cqn, ckvn, w_uq_p, w_uk_p, w_uv_p, l)
        lru_args = (lru_conv_w, conv_b, w_gate, b_gate, lru_lambda)
        ya_c, hl_c = _lru(xa, gg, h0_ctx, *lru_args, l, BATCH, SEQ, 0)
        ya_l, _ = _lru(xa, gg, state_lru[:, l], *lru_args, l, DEC_BATCH, DEC_SEQ, N_CTX // DEC_SEQ)
        yb_c = _attention(q, kt, v, None, n_seq=BATCH, seq_len=SEQ, row0=0, tq=SEQ,
                          n_groups=GQA_KV_HEADS, n_stack=GQA_GROUP, layer=l)
        yb_l = _attention(q, kt, v, (ktc, vcache), n_seq=DEC_BATCH, seq_len=DEC_SEQ, row0=N_CTX, tq=TQ_GQA,
                          n_groups=GQA_KV_HEADS, n_stack=GQA_GROUP, layer=l)
        yc_c = _attention(qc, kct, vc, None, n_seq=BATCH, seq_len=SEQ, row0=0, tq=SEQ,
                          n_groups=MLA_HEADS, n_stack=1, layer=l)
        yc_l = _attention(qc, kct, vc, (kctc, vcc), n_seq=DEC_BATCH, seq_len=DEC_SEQ, row0=N_CTX, tq=TQ_MLA,
                          n_groups=MLA_HEADS, n_stack=1, layer=l)
        x = _outproj(x, mod, (ya_c, ya_l), (yb_c, yb_l), (yc_c, yc_l), wo_a, wo_b, wo_c, ln_g, ln_b, l)
        x = _ffn(x, mod, wg, wu, wd, ln_g, ln_b, l, 2, 1)
        ctx_rows = ctx_out[:N_CTX]
        new_k.append(ctx_rows[:, 0:128].reshape(BATCH, SEQ, GQA_KV_HEADS, HEAD_DIM))
        new_v.append(ctx_rows[:, 128:256].reshape(BATCH, SEQ, GQA_KV_HEADS, HEAD_DIM))
        new_ckv.append(ctx_rows[:, 256:384].reshape(BATCH, SEQ, MLA_KV_RANK))
        new_kr.append(ctx_rows[:, 384 + MLA_ROPE_OFF:384 + MLA_ROPE_OFF + MLA_ROPE_DIM].reshape(BATCH, SEQ, MLA_ROPE_DIM))
        new_lru.append(hl_c)
    y_prompt = x[:N_CTX].reshape(BATCH, SEQ, D_MODEL)
    y_sample = x[N_CTX:].reshape(DEC_BATCH, DEC_SEQ, D_MODEL)
    return (y_prompt, y_sample, jnp.stack(new_k, axis=1), jnp.stack(new_v, axis=1),
            jnp.stack(new_ckv, axis=1), jnp.stack(new_kr, axis=1), jnp.stack(new_lru, axis=1))
```
